```python
import jax, jax.numpy as jnp
from jax import lax
import numpy as np

D_MODEL = 1024
BATCH = 2
SEQ = 16384
DEPTH = 2
DEC_BATCH = 16
DEC_SEQ = 32
PAST_LEN = 1024

CHUNK = 64
N_MIXERS = 2
N_LRU_LAYERS = (DEPTH + 1) // 2
N_POOL_LAYERS = DEPTH // 2
D_RNN = D_MODEL
N_GATE_BLOCKS = 16
GATE_BLOCK = D_RNN // N_GATE_BLOCKS
CONV_WIDTH = 4
LRU_C = 8.0
POOL_WINDOWS = (2, 4, 8, 16)
N_POOL_GROUPS = len(POOL_WINDOWS)
POOL_GROUP = D_MODEL // N_POOL_GROUPS
POOL_STATE = max(POOL_WINDOWS) - 1
D_FF = 2816
N_EXPERTS = 8
TOP_K = 2
EPS = 1e-6

kernel_name = 'hybrid_rglru_pool_moe_stream_step'


def rmsnorm(x, g):
    xf = x.astype(jnp.float32)
    y = xf * lax.rsqrt(jnp.mean(xf * xf, axis=-1, keepdims=True) + EPS)
    return (y * g.astype(jnp.float32)).astype(x.dtype)


def _lin_combine(c1, c2):
    a1, b1 = c1
    a2, b2 = c2
    return a1 * a2, a2 * b1 + b2


def rglru_mixer(xn, pos, conv_prev, h_prev, w_x, w_y, conv_w, conv_b, w_a, b_a, w_i, b_i, lam, w_out):
    B, L, _ = xn.shape
    f32 = jnp.float32
    gate = jax.nn.gelu(xn @ w_y, approximate=True)
    u = xn @ w_x
    upad = jnp.concatenate([conv_prev.astype(u.dtype), u], axis=1)
    conv = conv_b + upad[:, 0:L] * conv_w[0]
    for k in range(1, CONV_WIDTH):
        conv = conv + upad[:, k:k + L] * conv_w[k]
    new_conv = upad[:, -(CONV_WIDTH - 1):]
    cb = conv.reshape(B, L, N_GATE_BLOCKS, GATE_BLOCK)
    r = jax.nn.sigmoid((jnp.einsum('blhi,hij->blhj', cb, w_a).reshape(B, L, D_RNN) + b_a).astype(f32))
    i = jax.nn.sigmoid((jnp.einsum('blhi,hij->blhj', cb, w_i).reshape(B, L, D_RNN) + b_i).astype(f32))
    log_a = -LRU_C * r * jax.nn.softplus(-lam.astype(f32))
    a = jnp.exp(log_a)
    mult = jnp.sqrt(-jnp.expm1(2.0 * log_a))
    mult = jnp.where((pos == 0)[None, :, None], 1.0, mult)
    b = conv.astype(f32) * i * mult
    b = b.at[:, 0].add(a[:, 0] * h_prev.astype(f32))
    _, h = lax.associative_scan(_lin_combine, (a, b), axis=1)
    new_h = h[:, -1].astype(h_prev.dtype)
    y = (h.astype(xn.dtype) * gate) @ w_out
    return y, new_conv, new_h


def pool_mixer(xn, pos, pool_prev, w_g, b_g, scale):
    B, L, D = xn.shape
    f32 = jnp.float32
    xc = jnp.concatenate([pool_prev.astype(xn.dtype), xn], axis=1)
    xcf = xc.astype(f32)
    cs = jnp.concatenate([jnp.zeros((B, 1, D), f32), jnp.cumsum(xcf, axis=1)], axis=1)
    end = cs[:, POOL_STATE + 1:]
    pooled = []
    for g, w in enumerate(POOL_WINDOWS):
        sl = slice(g * POOL_GROUP, (g + 1) * POOL_GROUP)
        start = cs[:, POOL_STATE + 1 - w:POOL_STATE + 1 - w + L, sl]
        cnt = jnp.minimum(pos + 1, w).astype(f32)[None, :, None]
        pooled.append((end[..., sl] - start) / cnt)
    pooled = jnp.concatenate(pooled, axis=-1)
    d = (pooled - xcf[:, POOL_STATE:]).reshape(B, L, N_POOL_GROUPS, POOL_GROUP)
    y = jnp.einsum('blgi,gij->blgj', d, w_g.astype(f32)).reshape(B, L, D) + b_g.astype(f32)
    y = y * scale.astype(f32)
    new_pool = xc[:, -POOL_STATE:]
    return y.astype(xn.dtype), new_pool


def swiglu(xn, w_gate, w_up, w_down):
    return (jax.nn.silu(xn @ w_gate) * (xn @ w_up)) @ w_down


def moe_swiglu(xn, w_router, b_router, w_gate, w_up, w_down):
    B, L, D = xn.shape
    t = xn.reshape(B * L, D)
    logits = (t @ w_router).astype(jnp.float32) + b_router.astype(jnp.float32)
    top_v, top_i = lax.top_k(logits, TOP_K)
    probs = jax.nn.softmax(top_v, axis=-1)
    gates = jnp.sum(jax.nn.one_hot(top_i, N_EXPERTS, dtype=jnp.float32) * probs[..., None], axis=1)
    out = jnp.zeros((B * L, D), jnp.float32)
    for e in range(N_EXPERTS):
        he = jax.nn.silu(t @ w_gate[e]) * (t @ w_up[e])
        out = out + gates[:, e:e + 1] * (he @ w_down[e]).astype(jnp.float32)
    return out.astype(xn.dtype).reshape(B, L, D)


def trunk(x, pos, conv_in, rnn_in, pool_in,
          lru_norm_g, lru_w_x, lru_w_y, lru_conv_w, lru_conv_b, lru_w_a, lru_b_a, lru_w_i, lru_b_i,
          lru_lambda, lru_w_out, ffn_norm_g, ffn_w_gate, ffn_w_up, ffn_w_down,
          pool_norm_g, pool_w, pool_b, pool_scale,
          moe_norm_g, moe_w_router, moe_b_router, moe_w_gate, moe_w_up, moe_w_down, final_norm_g):
    convs, rnns, pools = [], [], []
    for layer in range(DEPTH):
        j = layer // N_MIXERS
        if layer % N_MIXERS == 0:
            y, c, h = rglru_mixer(rmsnorm(x, lru_norm_g[j]), pos, conv_in[j], rnn_in[j],
                                  lru_w_x[j], lru_w_y[j], lru_conv_w[j], lru_conv_b[j],
                                  lru_w_a[j], lru_b_a[j], lru_w_i[j], lru_b_i[j], lru_lambda[j], lru_w_out[j])
            convs.append(c)
            rnns.append(h)
            x = x + y
            x = x + swiglu(rmsnorm(x, ffn_norm_g[j]), ffn_w_gate[j], ffn_w_up[j], ffn_w_down[j])
        else:
            y, p = pool_mixer(rmsnorm(x, pool_norm_g[j]), pos, pool_in[j], pool_w[j], pool_b[j], pool_scale[j])
            pools.append(p)
            x = x + y
            x = x + moe_swiglu(rmsnorm(x, moe_norm_g[j]), moe_w_router[j], moe_b_router[j],
                               moe_w_gate[j], moe_w_up[j], moe_w_down[j])
    return rmsnorm(x, final_norm_g), jnp.stack(convs), jnp.stack(rnns), jnp.stack(pools)


def setup_inputs(seed: int = 0) -> dict:
    key = jax.random.key(seed)
    ks = iter(jax.random.split(key, 40))
    nrm = lambda shape, s: jax.random.normal(next(ks), shape, jnp.float32) * s
    na, nb = N_LRU_LAYERS, N_POOL_LAYERS
    u = jax.random.uniform(next(ks), (na, D_RNN), jnp.float32, minval=0.9, maxval=0.999)
    sig = u ** (1.0 / LRU_C)
    lam = jnp.log(sig) - jnp.log1p(-sig)
    return {
        'x_prompt': nrm((BATCH, SEQ, D_MODEL), 1.0),
        'x_sample': nrm((DEC_BATCH, DEC_SEQ, D_MODEL), 1.0),
        'cache_conv': nrm((na, DEC_BATCH, CONV_WIDTH - 1, D_RNN), 1.0),
        'state_rnn': nrm((na, DEC_BATCH, D_RNN), 0.5),
        'cache_pool': nrm((nb, DEC_BATCH, POOL_STATE, D_MODEL), 1.0),
        'lru_norm_g': 1.0 + nrm((na, D_MODEL), 0.05),
        'lru_w_x': nrm((na, D_MODEL, D_RNN), D_MODEL ** -0.5),
        'lru_w_y': nrm((na, D_MODEL, D_RNN), D_MODEL ** -0.5),
        'lru_conv_w': nrm((na, CONV_WIDTH, D_RNN), CONV_WIDTH ** -0.5),
        'lru_conv_b': nrm((na, D_RNN), 0.01),
        'lru_w_a': nrm((na, N_GATE_BLOCKS, GATE_BLOCK, GATE_BLOCK), GATE_BLOCK ** -0.5),
        'lru_b_a': nrm((na, D_RNN), 0.1),
        'lru_w_i': nrm((na, N_GATE_BLOCKS, GATE_BLOCK, GATE_BLOCK), GATE_BLOCK ** -0.5),
        'lru_b_i': nrm((na, D_RNN), 0.1),
        'lru_lambda': lam,
        'lru_w_out': nrm((na, D_RNN, D_MODEL), D_RNN ** -0.5),
        'ffn_norm_g': 1.0 + nrm((na, D_MODEL), 0.05),
        'ffn_w_gate': nrm((na, D_MODEL, D_FF), D_MODEL ** -0.5),
        'ffn_w_up': nrm((na, D_MODEL, D_FF), D_MODEL ** -0.5),
        'ffn_w_down': nrm((na, D_FF, D_MODEL), D_FF ** -0.5),
        'pool_norm_g': 1.0 + nrm((nb, D_MODEL), 0.05),
        'pool_w': nrm((nb, N_POOL_GROUPS, POOL_GROUP, POOL_GROUP), POOL_GROUP ** -0.5),
        'pool_b': nrm((nb, D_MODEL), 0.01),
        'pool_scale': 0.5 + nrm((nb, D_MODEL), 0.05),
        'moe_norm_g': 1.0 + nrm((nb, D_MODEL), 0.05),
        'moe_w_router': nrm((nb, D_MODEL, N_EXPERTS), D_MODEL ** -0.5),
        'moe_b_router': nrm((nb, N_EXPERTS), 0.01),
        'moe_w_gate': nrm((nb, N_EXPERTS, D_MODEL, D_FF), D_MODEL ** -0.5),
        'moe_w_up': nrm((nb, N_EXPERTS, D_MODEL, D_FF), D_MODEL ** -0.5),
        'moe_w_down': nrm((nb, N_EXPERTS, D_FF, D_MODEL), D_FF ** -0.5),
        'final_norm_g': 1.0 + nrm((D_MODEL,), 0.05),
    }


def reference(x_prompt, x_sample, cache_conv, state_rnn, cache_pool,
              lru_norm_g, lru_w_x, lru_w_y, lru_conv_w, lru_conv_b, lru_w_a, lru_b_a, lru_w_i, lru_b_i,
              lru_lambda, lru_w_out, ffn_norm_g, ffn_w_gate, ffn_w_up, ffn_w_down,
              pool_norm_g, pool_w, pool_b, pool_scale,
              moe_norm_g, moe_w_router, moe_b_router, moe_w_gate, moe_w_up, moe_w_down, final_norm_g):
    weights = (lru_norm_g, lru_w_x, lru_w_y, lru_conv_w, lru_conv_b, lru_w_a, lru_b_a, lru_w_i, lru_b_i,
               lru_lambda, lru_w_out, ffn_norm_g, ffn_w_gate, ffn_w_up, ffn_w_down,
               pool_norm_g, pool_w, pool_b, pool_scale,
               moe_norm_g, moe_w_router, moe_b_router, moe_w_gate, moe_w_up, moe_w_down, final_norm_g)
    bp, lp, _ = x_prompt.shape
    bs, ls, _ = x_sample.shape
    dt = x_prompt.dtype
    pos_p = jnp.arange(lp, dtype=jnp.int32)
    conv0 = jnp.zeros((N_LRU_LAYERS, bp, CONV_WIDTH - 1, D_RNN), dt)
    rnn0 = jnp.zeros((N_LRU_LAYERS, bp, D_RNN), dt)
    pool0 = jnp.zeros((N_POOL_LAYERS, bp, POOL_STATE, D_MODEL), dt)
    y_prompt, conv_prompt, rnn_prompt, pool_prompt = trunk(x_prompt, pos_p, conv0, rnn0, pool0, *weights)
    pos_s = PAST_LEN + jnp.arange(ls, dtype=jnp.int32)
    y_sample, conv_sample, rnn_sample, pool_sample = trunk(x_sample, pos_s, cache_conv, state_rnn, cache_pool, *weights)
    return (y_prompt, y_sample, conv_prompt, conv_sample, rnn_prompt, rnn_sample, pool_prompt, pool_sample)
```

```python
import functools

import jax
import jax.numpy as jnp
from jax import lax
from jax.experimental import pallas as pl
from jax.experimental.pallas import tpu as pltpu

EPS = 1e-6
LRU_C = 8.0
CONV_WIDTH = 4
POOL_WINDOWS = (2, 4, 8, 16)
POOL_STATE = max(POOL_WINDOWS) - 1
TOP_K = 2
PAST_LEN = 1024

V7X_LANES = 128
V7X_SUBLANES = 8
V7X_MXU_DIM = 256
V7X_VMEM_LIMIT_BYTES = 56 * 1024 * 1024

F32 = jnp.float32
BF16 = jnp.bfloat16


def _rmsnorm(x, g):
    return x * lax.rsqrt(jnp.mean(x * x, axis=-1, keepdims=True) + EPS) * g


def _divmod_const(v, c):
    if c & (c - 1) == 0:
        return lax.shift_right_logical(v, c.bit_length() - 1), v & (c - 1)
    return v // c, v % c


def _bdot(a, b):
    return jnp.dot(a, b, preferred_element_type=F32)


def _ff_chunks(d_ff):
    step = 4 * V7X_MXU_DIM
    return [(c, min(c + step, d_ff)) for c in range(0, d_ff, step)]


def _swiglu(xb, wg, wu, wd, d_ff):
    acc = None
    for c0, c1 in _ff_chunks(d_ff):
        g = _bdot(xb, wg(c0, c1))
        u = _bdot(xb, wu(c0, c1))
        h = (jax.nn.silu(g) * u).astype(BF16)
        p = _bdot(h, wd(c0, c1))
        acc = p if acc is None else acc + p
    return acc


def _lru_kernel(x_ref, cprev_ref, hprev_ref, g_ref, wxy_ref, cw_ref, cb_ref, wai_ref, ba_ref, bi_ref,
                lam_ref, wout_ref, xo_ref, conv_o_ref, h_o_ref, ubuf, hstate, hbuf, *, S, Ls, pos0):
    t = pl.program_id(1)
    D = x_ref.shape[-1]
    R = wout_ref.shape[0]
    M = S * Ls
    HB = V7X_SUBLANES
    NH = CONV_WIDTH - 1

    @pl.when(t == 0)
    def _():
        ubuf[:, HB - NH:HB, :] = cprev_ref[...]
        hstate[...] = hprev_ref[...]

    x = x_ref[...].reshape(M, D)
    xn = _rmsnorm(x, g_ref[...])
    uy = _bdot(xn.astype(BF16), wxy_ref[...])
    u = uy[:, :R]
    gate = jax.nn.gelu(uy[:, R:], approximate=True)

    ubuf[:, HB:HB + Ls, :] = u.reshape(S, Ls, R)
    cw = cw_ref[...]
    conv = cb_ref[...] + ubuf[:, HB - NH:HB - NH + Ls, :].reshape(M, R) * cw[0:1]
    for k in range(1, NH):
        conv = conv + ubuf[:, HB - NH + k:HB - NH + k + Ls, :].reshape(M, R) * cw[k:k + 1]
    conv = conv + u * cw[NH:NH + 1]
    tail = ubuf[:, HB + Ls - NH:HB + Ls, :]
    conv_o_ref[...] = tail
    ubuf[:, HB - NH:HB, :] = tail

    nq, Q = wai_ref.shape[0], wai_ref.shape[1]
    cbf = conv.astype(BF16)
    pre = [_bdot(cbf[:, q * Q:(q + 1) * Q], wai_ref[q]) for q in range(nq)]
    r = jax.nn.sigmoid(jnp.concatenate([p[:, :Q] for p in pre], axis=1) + ba_ref[...])
    i = jax.nn.sigmoid(jnp.concatenate([p[:, Q:] for p in pre], axis=1) + bi_ref[...])
    log_a = -LRU_C * r * jax.nn.softplus(-lam_ref[...])
    a = jnp.exp(log_a)
    mult = jnp.sqrt(-jnp.tanh(log_a) * (a * a + 1.0))
    if pos0 == 0:
        row = lax.broadcasted_iota(jnp.int32, (M, R), 0)
        lrow = _divmod_const(row, Ls)[1] if S > 1 else row
        mult = jnp.where(lrow + t * Ls == 0, 1.0, mult)
    b = conv * i * mult

    sub = _divmod_const(lax.broadcasted_iota(jnp.int32, (M, R), 0), V7X_SUBLANES)[1]
    s = 1
    while s < V7X_SUBLANES:
        keep = sub >= s
        a_sh = pltpu.roll(a, s, axis=0)
        b_sh = pltpu.roll(b, s, axis=0)
        b = jnp.where(keep, a * b_sh + b, b)
        a = jnp.where(keep, a * a_sh, a)
        s *= 2
    for si in range(S):
        h_in = hstate[si]
        for gi in range(Ls // V7X_SUBLANES):
            g0 = si * Ls + gi * V7X_SUBLANES
            hh = a[g0:g0 + V7X_SUBLANES] * h_in + b[g0:g0 + V7X_SUBLANES]
            hbuf[g0:g0 + V7X_SUBLANES, :] = hh
            h_in = hh[V7X_SUBLANES - 1:V7X_SUBLANES]
        hstate[si] = h_in
    h_o_ref[...] = hstate[...]

    y = _bdot((hbuf[...] * gate).astype(BF16), wout_ref[...])
    xo_ref[...] = (x + y).reshape(S, Ls, D)


def _lru_block(x, conv_prev, h_prev, p, *, S, Ls, pos0):
    B, L, D = x.shape
    R = p["wout"].shape[0]
    NH = CONV_WIDTH - 1
    assert B % S == 0 and L % Ls == 0 and Ls % V7X_SUBLANES == 0 and Ls >= NH
    const = lambda shape: pl.BlockSpec(shape, lambda b, t: (0,) * len(shape))
    kern = functools.partial(_lru_kernel, S=S, Ls=Ls, pos0=pos0)
    return pl.pallas_call(
        kern,
        grid=(B // S, L // Ls),
        in_specs=[
            pl.BlockSpec((S, Ls, D), lambda b, t: (b, t, 0)),
            pl.BlockSpec((S, NH, R), lambda b, t: (b, 0, 0)),
            pl.BlockSpec((S, 1, R), lambda b, t: (b, 0, 0)),
            const((1, D)), const(p["wxy"].shape), const((CONV_WIDTH, R)), const((1, R)),
            const(p["wai"].shape), const((1, R)), const((1, R)), const((1, R)), const(p["wout"].shape),
        ],
        out_specs=[
            pl.BlockSpec((S, Ls, D), lambda b, t: (b, t, 0)),
            pl.BlockSpec((S, NH, R), lambda b, t: (b, 0, 0)),
            pl.BlockSpec((S, 1, R), lambda b, t: (b, 0, 0)),
        ],
        out_shape=[
            jax.ShapeDtypeStruct((B, L, D), F32),
            jax.ShapeDtypeStruct((B, NH, R), F32),
            jax.ShapeDtypeStruct((B, 1, R), F32),
        ],
        scratch_shapes=[
            pltpu.VMEM((S, V7X_SUBLANES + Ls, R), F32),
            pltpu.VMEM((S, 1, R), F32),
            pltpu.VMEM((S * Ls, R), F32),
        ],
        compiler_params=pltpu.CompilerParams(
            dimension_semantics=("arbitrary", "arbitrary"), vmem_limit_bytes=V7X_VMEM_LIMIT_BYTES),
        name="lru_block",
    )(x, conv_prev, h_prev.reshape(B, 1, R), p["g"], p["wxy"], p["cw"], p["cb"], p["wai"], p["ba"], p["bi"],
      p["lam"], p["wout"])


def _ffn_kernel(x_ref, g_ref, wg_ref, wu_ref, wd_ref, o_ref):
    x = x_ref[...]
    xb = _rmsnorm(x, g_ref[...]).astype(BF16)
    y = _swiglu(xb, lambda a, b: wg_ref[:, a:b], lambda a, b: wu_ref[:, a:b], lambda a, b: wd_ref[a:b, :],
                wg_ref.shape[1])
    o_ref[...] = x + y


def _ffn_block(x, p, *, T):
    N, D = x.shape
    Fd = p["wg"].shape[1]
    assert N % T == 0
    const = lambda shape: pl.BlockSpec(shape, lambda i: (0,) * len(shape), pipeline_mode=pl.Buffered(1))
    return pl.pallas_call(
        _ffn_kernel,
        grid=(N // T,),
        in_specs=[pl.BlockSpec((T, D), lambda i: (i, 0)), const((1, D)),
                  const((D, Fd)), const((D, Fd)), const((Fd, D))],
        out_specs=pl.BlockSpec((T, D), lambda i: (i, 0)),
        out_shape=jax.ShapeDtypeStruct((N, D), F32),
        compiler_params=pltpu.CompilerParams(
            dimension_semantics=("arbitrary",), vmem_limit_bytes=V7X_VMEM_LIMIT_BYTES),
        name="ffn_block",
    )(x, p["g"], p["wg"], p["wu"], p["wd"])


def _pool_kernel(x_ref, pprev_ref, g_ref, wp_ref, bg_ref, sc_ref, mg_ref, wr_ref, br_ref,
                 xo_ref, xn_o_ref, ridx_ref, rp_ref, pool_o_ref, xbuf, *, S, Ls, H, n_t, pos0, n_exp):
    t = pl.program_id(1)
    D = x_ref.shape[-1]
    M = S * Ls
    E = H + Ls
    G, P = wp_ref.shape[0], wp_ref.shape[1]

    HS = pprev_ref.shape[1]

    @pl.when(t == 0)
    def _():
        if H > HS:
            xbuf[:, 0:H - HS, :] = jnp.zeros((S, H - HS, D), F32)
        xbuf[:, H - HS:H, :] = pprev_ref[...]

    x = x_ref[...].reshape(M, D)
    xn = _rmsnorm(x, g_ref[...])
    xbuf[:, H:E, :] = xn.reshape(S, Ls, D)
    pool_o_ref[...] = xbuf[:, E - HS:E, :]

    ext = xbuf[...].reshape(S * E, D)
    if n_t > 1:
        xbuf[:, 0:H, :] = xbuf[:, E - H:E, :]
    hi = ext.astype(BF16)
    lo = (ext - hi.astype(F32)).astype(BF16)

    if S == 1:
        Mb = min(Ls, H)
        blocks = [(m0, m0, Mb + H) for m0 in range(0, M, Mb)]
    else:
        Mb = M
        blocks = [(0, 0, S * E)]
    mi = lax.broadcasted_iota(jnp.int32, (Mb, blocks[0][2]), 0)
    ci = lax.broadcasted_iota(jnp.int32, (Mb, blocks[0][2]), 1)
    if S == 1:
        dist = mi + H - ci
    else:
        (ms, mt), (cs, cr) = _divmod_const(mi, Ls), _divmod_const(ci, E)
        dist = jnp.where(ms == cs, mt + H - cr, -1)

    row = lax.broadcasted_iota(jnp.int32, (M, 1), 0)
    pos = pos0 + t * Ls + (_divmod_const(row, Ls)[1] if S > 1 else row)
    ys = []
    for gi, w in enumerate(POOL_WINDOWS[:G]):
        band = jnp.where((dist >= 0) & (dist < w), 1.0, 0.0).astype(BF16)
        sl = slice(gi * P, (gi + 1) * P)
        sums = [_bdot(band, hi[c0:c0 + cb, sl]) + _bdot(band, lo[c0:c0 + cb, sl]) for _, c0, cb in blocks]
        sg = sums[0] if len(sums) == 1 else jnp.concatenate(sums, axis=0)
        cnt = jnp.minimum(pos + 1, w).astype(F32)
        d = sg / cnt - xn[:, sl]
        ys.append(_bdot(d.astype(BF16), wp_ref[gi]))
    y = (jnp.concatenate(ys, axis=1) + bg_ref[...]) * sc_ref[...]
    x2 = x + y
    xo_ref[...] = x2.reshape(S, Ls, D)

    xn2 = _rmsnorm(x2, mg_ref[...]).astype(BF16)
    xn_o_ref[...] = xn2.reshape(S, Ls, D)
    NL = wr_ref.shape[1]
    lane = lax.broadcasted_iota(jnp.int32, (M, NL), 1)
    logits = jnp.where(lane < n_exp, _bdot(xn2, wr_ref[...]) + br_ref[...], -jnp.inf)
    v0 = jnp.max(logits, axis=1, keepdims=True)
    i0 = jnp.min(jnp.where(logits == v0, lane, NL), axis=1, keepdims=True)
    rest = jnp.where(lane == i0, -jnp.inf, logits)
    v1 = jnp.max(rest, axis=1, keepdims=True)
    i1 = jnp.min(jnp.where(rest == v1, lane, NL), axis=1, keepdims=True)
    e1 = jnp.exp(v1 - v0)
    den = 1.0 + e1
    ridx_ref[...] = jnp.where(lane == 0, i0, jnp.where(lane == 1, i1, 0)).reshape(S, Ls, NL)
    rp_ref[...] = jnp.where(lane == 0, 1.0 / den, jnp.where(lane == 1, e1 / den, 0.0)).reshape(S, Ls, NL)


def _pool_block(x, pool_prev, p, *, S, Ls, pos0):
    B, L, D = x.shape
    assert B % S == 0 and L % Ls == 0 and Ls % V7X_SUBLANES == 0
    HS = 2 * V7X_SUBLANES
    assert POOL_STATE < HS
    H = V7X_LANES if S == 1 else pl.next_power_of_2(Ls + HS) - Ls
    assert H >= HS and (S * (H + Ls)) % V7X_LANES == 0
    NL = p["wr"].shape[1]
    pprev = jnp.pad(pool_prev, ((0, 0), (HS - POOL_STATE, 0), (0, 0)))
    const = lambda shape: pl.BlockSpec(shape, lambda b, t: (0,) * len(shape))
    kern = functools.partial(_pool_kernel, S=S, Ls=Ls, H=H, n_t=L // Ls, pos0=pos0, n_exp=p["n_exp"])
    tok = lambda last: pl.BlockSpec((S, Ls, last), lambda b, t: (b, t, 0))
    x2, xn2, ridx, rp, pool_new = pl.pallas_call(
        kern,
        grid=(B // S, L // Ls),
        in_specs=[tok(D), pl.BlockSpec((S, HS, D), lambda b, t: (b, 0, 0)),
                  const((1, D)), const(p["wp"].shape), const((1, D)), const((1, D)), const((1, D)),
                  const(p["wr"].shape), const((1, NL))],
        out_specs=[tok(D), tok(D), tok(NL), tok(NL), pl.BlockSpec((S, HS, D), lambda b, t: (b, 0, 0))],
        out_shape=[
            jax.ShapeDtypeStruct((B, L, D), F32),
            jax.ShapeDtypeStruct((B, L, D), BF16),
            jax.ShapeDtypeStruct((B, L, NL), jnp.int32),
            jax.ShapeDtypeStruct((B, L, NL), F32),
            jax.ShapeDtypeStruct((B, HS, D), F32),
        ],
        scratch_shapes=[pltpu.VMEM((S, H + Ls, D), F32)],
        compiler_params=pltpu.CompilerParams(
            dimension_semantics=("arbitrary", "arbitrary"), vmem_limit_bytes=V7X_VMEM_LIMIT_BYTES),
        name="pool_block",
    )(x, pprev, p["g"], p["wp"], p["bg"], p["sc"], p["mg"], p["wr"], p["br"])
    return x2, xn2, ridx, rp, pool_new[:, HS - POOL_STATE:, :]


def _moe_dense_kernel(xn_ref, ridx_ref, rp_ref, wg_ref, wu_ref, wd_ref, o_ref):
    e = pl.program_id(1)
    ridx = ridx_ref[...]
    rp = rp_ref[...]
    gate = jnp.zeros((ridx.shape[0], 1), F32)
    for k in range(TOP_K):
        gate = gate + jnp.where(ridx[:, k:k + 1] == e, rp[:, k:k + 1], 0.0)
    y = _swiglu(xn_ref[...], lambda a, b: wg_ref[0, :, a:b], lambda a, b: wu_ref[0, :, a:b],
                lambda a, b: wd_ref[0, a:b, :], wg_ref.shape[2])

    @pl.when(e == 0)
    def _():
        o_ref[...] = gate * y

    @pl.when(e > 0)
    def _():
        o_ref[...] += gate * y


def _moe_dense(xn, ridx, rp, p, *, T):
    N, D = xn.shape
    n_exp, _, Fd = p["wg"].shape
    NL = ridx.shape[1]
    assert N % T == 0
    wspec = lambda shape: pl.BlockSpec((1,) + shape, lambda i, e: (e, 0, 0))
    tok = lambda last: pl.BlockSpec((T, last), lambda i, e: (i, 0))
    return pl.pallas_call(
        _moe_dense_kernel,
        grid=(N // T, n_exp),
        in_specs=[tok(D), tok(NL), tok(NL), wspec((D, Fd)), wspec((D, Fd)), wspec((Fd, D))],
        out_specs=tok(D),
        out_shape=jax.ShapeDtypeStruct((N, D), F32),
        compiler_params=pltpu.CompilerParams(
            dimension_semantics=("arbitrary", "arbitrary"), vmem_limit_bytes=V7X_VMEM_LIMIT_BYTES),
        name="moe_dense",
    )(xn, ridx, rp, p["wg"], p["wu"], p["wd"])


def _final_kernel(x_ref, y_ref, g_ref, o_ref):
    o_ref[...] = _rmsnorm(x_ref[...] + y_ref[...], g_ref[...])


def _final_block(x, y, g, *, T):
    N, D = x.shape
    assert N % T == 0
    tok = pl.BlockSpec((T, D), lambda i: (i, 0))
    return pl.pallas_call(
        _final_kernel,
        grid=(N // T,),
        in_specs=[tok, tok, pl.BlockSpec((1, D), lambda i: (0, 0))],
        out_specs=tok,
        out_shape=jax.ShapeDtypeStruct((N, D), F32),
        compiler_params=pltpu.CompilerParams(
            dimension_semantics=("arbitrary",), vmem_limit_bytes=V7X_VMEM_LIMIT_BYTES),
        name="final_norm",
    )(x, y, g)


def _block_diag_groups(w, group):
    n, k, _ = w.shape
    eye = jnp.eye(group, dtype=w.dtype)
    wg = w.reshape(n // group, group, k, k)
    return jnp.einsum("qaij,ab->qaibj", wg, eye).reshape(n // group, group * k, group * k)


def _row(v):
    return v.reshape(1, -1).astype(F32)


def _prep_lru(j, norm_g, w_x, w_y, conv_w, conv_b, w_a, b_a, w_i, b_i, lam, w_out):
    nblk, blk = w_a.shape[1], w_a.shape[2]
    group = max(1, min(nblk, V7X_MXU_DIM // blk))
    assert nblk % group == 0
    wai = jnp.concatenate([_block_diag_groups(w_a[j], group), _block_diag_groups(w_i[j], group)], axis=2)
    return dict(g=_row(norm_g[j]), wxy=jnp.concatenate([w_x[j], w_y[j]], axis=1).astype(BF16),
                cw=conv_w[j].astype(F32), cb=_row(conv_b[j]), wai=wai.astype(BF16), ba=_row(b_a[j]),
                bi=_row(b_i[j]), lam=_row(lam[j]), wout=w_out[j].astype(BF16))


def _prep_pool(j, norm_g, w, b, scale, moe_g, w_router, b_router):
    D, n_exp = w_router.shape[1], w_router.shape[2]
    NL = V7X_LANES
    assert n_exp <= NL
    wr = jnp.zeros((D, NL), BF16).at[:, :n_exp].set(w_router[j].astype(BF16))
    br = jnp.zeros((1, NL), F32).at[:, :n_exp].set(b_router[j].astype(F32))
    return dict(g=_row(norm_g[j]), wp=w[j].astype(BF16), bg=_row(b[j]), sc=_row(scale[j]), mg=_row(moe_g[j]),
                wr=wr, br=br, n_exp=n_exp)


def _largest_tile(n, cap):
    t = cap
    while n % t:
        t //= 2
    return t


def _trunk(x, pos0, conv_in, rnn_in, pool_in, lru, ffn, pool, moe, final_g):
    B, L, D = x.shape
    S, Ls = (B, L) if B * L <= 512 else (1, _largest_tile(L, 256))
    x1, conv_new, h_new = _lru_block(x, conv_in, rnn_in, lru, S=S, Ls=Ls, pos0=pos0)
    N = B * L
    T = _largest_tile(N, 512)
    x1 = _ffn_block(x1.reshape(N, D), ffn, T=T).reshape(B, L, D)
    x2, xn2, ridx, rp, pool_new = _pool_block(x1, pool_in, pool, S=S, Ls=Ls, pos0=pos0)
    NL = ridx.shape[-1]
    y = _moe_dense(xn2.reshape(N, D), ridx.reshape(N, NL), rp.reshape(N, NL), moe, T=T)
    out = _final_block(x2.reshape(N, D), y, final_g, T=T).reshape(B, L, D)
    return out, conv_new, h_new.reshape(B, -1), pool_new


def kernel(x_prompt, x_sample, cache_conv, state_rnn, cache_pool, lru_norm_g, lru_w_x, lru_w_y, lru_conv_w, lru_conv_b, lru_w_a, lru_b_a, lru_w_i, lru_b_i, lru_lambda, lru_w_out, ffn_norm_g, ffn_w_gate, ffn_w_up, ffn_w_down, pool_norm_g, pool_w, pool_b, pool_scale, moe_norm_g, moe_w_router, moe_b_router, moe_w_gate, moe_w_up, moe_w_down, final_norm_g):
    assert lru_norm_g.shape[0] == 1 and pool_norm_g.shape[0] == 1, "one layer of each mixer type"
    bp, lp, D = x_prompt.shape
    lru = _prep_lru(0, lru_norm_g, lru_w_x, lru_w_y, lru_conv_w, lru_conv_b, lru_w_a, lru_b_a, lru_w_i, lru_b_i,
                    lru_lambda, lru_w_out)
    ffn = dict(g=_row(ffn_norm_g[0]), wg=ffn_w_gate[0].astype(BF16), wu=ffn_w_up[0].astype(BF16),
               wd=ffn_w_down[0].astype(BF16))
    pool = _prep_pool(0, pool_norm_g, pool_w, pool_b, pool_scale, moe_norm_g, moe_w_router, moe_b_router)
    moe = dict(wg=moe_w_gate[0].astype(BF16), wu=moe_w_up[0].astype(BF16), wd=moe_w_down[0].astype(BF16))
    fg = _row(final_norm_g)

    R = lru_w_out.shape[1]
    zc = jnp.zeros((bp, CONV_WIDTH - 1, R), F32)
    zh = jnp.zeros((bp, R), F32)
    zp = jnp.zeros((bp, POOL_STATE, D), F32)
    yp, cp, hp, pp = _trunk(x_prompt, 0, zc, zh, zp, lru, ffn, pool, moe, fg)
    ys, cs, hs, ps = _trunk(x_sample, PAST_LEN, cache_conv[0], state_rnn[0], cache_pool[0], lru, ffn, pool, moe, fg)
    return (yp, ys, cp[None], cs[None], hp[None], hs[None], pp[None], ps[None])
```

```python
import functools

import jax
import jax.numpy as jnp
from jax import lax
from jax.experimental import pallas as pl
from jax.experimental.pallas import tpu as pltpu

EPS = 1e-6
LRU_C = 8.0
CONV_WIDTH = 4
POOL_WINDOWS = (2, 4, 8, 16)
POOL_STATE = max(POOL_WINDOWS) - 1
TOP_K = 2
PAST_LEN = 1024

V7X_LANES = 128
V7X_SUBLANES = 8
V7X_MXU_DIM = 256
V7X_VMEM_LIMIT_BYTES = 56 * 1024 * 1024

F32 = jnp.float32
BF16 = jnp.bfloat16


def _rmsnorm(x, g):
    return x * lax.rsqrt(jnp.mean(x * x, axis=-1, keepdims=True) + EPS) * g


def _divmod_const(v, c):
    if c & (c - 1) == 0:
        return lax.shift_right_logical(v, c.bit_length() - 1), v & (c - 1)
    return v // c, v % c


def _bdot(a, b):
    return jnp.dot(a, b, preferred_element_type=F32)


def _put_token_tiles(ref, lead, val):
    for j in range(V7X_SUBLANES):
        ref[lead + (slice(None), j, slice(None))] = val[:, j * V7X_LANES:(j + 1) * V7X_LANES]


def _get_token_tiles(ref, lead):
    return jnp.concatenate([ref[lead + (slice(None), j, slice(None))] for j in range(V7X_SUBLANES)], axis=1)


def _ff_chunks(d_ff):
    step = 4 * V7X_MXU_DIM
    return [(c, min(c + step, d_ff)) for c in range(0, d_ff, step)]


def _swiglu(xb, wg, wu, wd, d_ff):
    acc = None
    for c0, c1 in _ff_chunks(d_ff):
        g = _bdot(xb, wg(c0, c1))
        u = _bdot(xb, wu(c0, c1))
        h = (jax.nn.silu(g) * u).astype(BF16)
        p = _bdot(h, wd(c0, c1))
        acc = p if acc is None else acc + p
    return acc


def _lru_kernel(x_ref, cprev_ref, hprev_ref, g_ref, wxy_ref, cw_ref, cb_ref, wai_ref, ba_ref, bi_ref,
                lam_ref, wout_ref, xo_ref, conv_o_ref, h_o_ref, ubuf, hstate, hbuf, *, S, Ls, pos0):
    t = pl.program_id(1)
    D = x_ref.shape[-1]
    R = wout_ref.shape[0]
    M = S * Ls
    HB = V7X_SUBLANES
    NH = CONV_WIDTH - 1

    @pl.when(t == 0)
    def _():
        ubuf[:, HB - NH:HB, :] = cprev_ref[...]
        hstate[...] = hprev_ref[...]

    x = x_ref[...].reshape(M, D)
    xn = _rmsnorm(x, g_ref[...])
    uy = _bdot(xn.astype(BF16), wxy_ref[...])
    u = uy[:, :R]
    gate = jax.nn.gelu(uy[:, R:], approximate=True)

    ubuf[:, HB:HB + Ls, :] = u.reshape(S, Ls, R)
    cw = cw_ref[...]
    conv = cb_ref[...] + ubuf[:, HB - NH:HB - NH + Ls, :].reshape(M, R) * cw[0:1]
    for k in range(1, NH):
        conv = conv + ubuf[:, HB - NH + k:HB - NH + k + Ls, :].reshape(M, R) * cw[k:k + 1]
    conv = conv + u * cw[NH:NH + 1]
    tail = ubuf[:, HB + Ls - NH:HB + Ls, :]
    conv_o_ref[...] = tail
    ubuf[:, HB - NH:HB, :] = tail

    nq, Q = wai_ref.shape[0], wai_ref.shape[1]
    cbf = conv.astype(BF16)
    pre = [_bdot(cbf[:, q * Q:(q + 1) * Q], wai_ref[q]) for q in range(nq)]
    r = jax.nn.sigmoid(jnp.concatenate([p[:, :Q] for p in pre], axis=1) + ba_ref[...])
    i = jax.nn.sigmoid(jnp.concatenate([p[:, Q:] for p in pre], axis=1) + bi_ref[...])
    log_a = -LRU_C * r * jax.nn.softplus(-lam_ref[...])
    a = jnp.exp(log_a)
    mult = jnp.sqrt(-jnp.tanh(log_a) * (a * a + 1.0))
    if pos0 == 0:
        row = lax.broadcasted_iota(jnp.int32, (M, R), 0)
        lrow = _divmod_const(row, Ls)[1] if S > 1 else row
        mult = jnp.where(lrow + t * Ls == 0, 1.0, mult)
    b = conv * i * mult

    sub = _divmod_const(lax.broadcasted_iota(jnp.int32, (M, R), 0), V7X_SUBLANES)[1]
    s = 1
    while s < V7X_SUBLANES:
        keep = sub >= s
        a_sh = pltpu.roll(a, s, axis=0)
        b_sh = pltpu.roll(b, s, axis=0)
        b = jnp.where(keep, a * b_sh + b, b)
        a = jnp.where(keep, a * a_sh, a)
        s *= 2
    for si in range(S):
        h_in = hstate[si]
        for gi in range(Ls // V7X_SUBLANES):
            g0 = si * Ls + gi * V7X_SUBLANES
            hh = a[g0:g0 + V7X_SUBLANES] * h_in + b[g0:g0 + V7X_SUBLANES]
            hbuf[g0:g0 + V7X_SUBLANES, :] = hh
            h_in = hh[V7X_SUBLANES - 1:V7X_SUBLANES]
        hstate[si] = h_in
    h_o_ref[...] = hstate[...]

    y = _bdot((hbuf[...] * gate).astype(BF16), wout_ref[...])
    xo_ref[...] = (x + y).reshape(S, Ls, D)


def _lru_block(x, conv_prev, h_prev, p, *, S, Ls, pos0):
    B, L, D = x.shape
    R = p["wout"].shape[0]
    NH = CONV_WIDTH - 1
    assert B % S == 0 and L % Ls == 0 and Ls % V7X_SUBLANES == 0 and Ls >= NH
    const = lambda shape: pl.BlockSpec(shape, lambda b, t: (0,) * len(shape))
    kern = functools.partial(_lru_kernel, S=S, Ls=Ls, pos0=pos0)
    return pl.pallas_call(
        kern,
        grid=(B // S, L // Ls),
        in_specs=[
            pl.BlockSpec((S, Ls, D), lambda b, t: (b, t, 0)),
            pl.BlockSpec((S, NH, R), lambda b, t: (b, 0, 0)),
            pl.BlockSpec((S, 1, R), lambda b, t: (b, 0, 0)),
            const((1, D)), const(p["wxy"].shape), const((CONV_WIDTH, R)), const((1, R)),
            const(p["wai"].shape), const((1, R)), const((1, R)), const((1, R)), const(p["wout"].shape),
        ],
        out_specs=[
            pl.BlockSpec((S, Ls, D), lambda b, t: (b, t, 0)),
            pl.BlockSpec((S, NH, R), lambda b, t: (b, 0, 0)),
            pl.BlockSpec((S, 1, R), lambda b, t: (b, 0, 0)),
        ],
        out_shape=[
            jax.ShapeDtypeStruct((B, L, D), F32),
            jax.ShapeDtypeStruct((B, NH, R), F32),
            jax.ShapeDtypeStruct((B, 1, R), F32),
        ],
        scratch_shapes=[
            pltpu.VMEM((S, V7X_SUBLANES + Ls, R), F32),
            pltpu.VMEM((S, 1, R), F32),
            pltpu.VMEM((S * Ls, R), F32),
        ],
        compiler_params=pltpu.CompilerParams(
            dimension_semantics=("arbitrary", "arbitrary"), vmem_limit_bytes=V7X_VMEM_LIMIT_BYTES),
        name="lru_block",
    )(x, conv_prev, h_prev.reshape(B, 1, R), p["g"], p["wxy"], p["cw"], p["cb"], p["wai"], p["ba"], p["bi"],
      p["lam"], p["wout"])


def _ffn_kernel(x_ref, g_ref, wg_ref, wu_ref, wd_ref, o_ref):
    x = x_ref[...]
    xb = _rmsnorm(x, g_ref[...]).astype(BF16)
    y = _swiglu(xb, lambda a, b: wg_ref[:, a:b], lambda a, b: wu_ref[:, a:b], lambda a, b: wd_ref[a:b, :],
                wg_ref.shape[1])
    o_ref[...] = x + y


def _ffn_block(x, p, *, T):
    N, D = x.shape
    Fd = p["wg"].shape[1]
    assert N % T == 0
    const = lambda shape: pl.BlockSpec(shape, lambda i: (0,) * len(shape), pipeline_mode=pl.Buffered(1))
    return pl.pallas_call(
        _ffn_kernel,
        grid=(N // T,),
        in_specs=[pl.BlockSpec((T, D), lambda i: (i, 0)), const((1, D)),
                  const((D, Fd)), const((D, Fd)), const((Fd, D))],
        out_specs=pl.BlockSpec((T, D), lambda i: (i, 0)),
        out_shape=jax.ShapeDtypeStruct((N, D), F32),
        compiler_params=pltpu.CompilerParams(
            dimension_semantics=("arbitrary",), vmem_limit_bytes=V7X_VMEM_LIMIT_BYTES),
        name="ffn_block",
    )(x, p["g"], p["wg"], p["wu"], p["wd"])


def _pool_kernel(x_ref, pprev_ref, g_ref, wp_ref, bg_ref, sc_ref, mg_ref, wr_ref, br_ref,
                 xo_ref, xn_o_ref, ridx_ref, rp_ref, pool_o_ref, xbuf, *, S, Ls, H, n_t, pos0, n_exp):
    t = pl.program_id(1)
    D = x_ref.shape[-1]
    M = S * Ls
    E = H + Ls
    G, P = wp_ref.shape[0], wp_ref.shape[1]

    HS = pprev_ref.shape[1]

    @pl.when(t == 0)
    def _():
        if H > HS:
            xbuf[:, 0:H - HS, :] = jnp.zeros((S, H - HS, D), F32)
        xbuf[:, H - HS:H, :] = pprev_ref[...]

    x = x_ref[...].reshape(M, D)
    xn = _rmsnorm(x, g_ref[...])
    xbuf[:, H:E, :] = xn.reshape(S, Ls, D)
    pool_o_ref[...] = xbuf[:, E - HS:E, :]

    ext = xbuf[...].reshape(S * E, D)
    if n_t > 1:
        xbuf[:, 0:H, :] = xbuf[:, E - H:E, :]
    hi = ext.astype(BF16)
    lo = (ext - hi.astype(F32)).astype(BF16)

    if S == 1:
        Mb = min(Ls, H)
        blocks = [(m0, m0, Mb + H) for m0 in range(0, M, Mb)]
    else:
        Mb = M
        blocks = [(0, 0, S * E)]
    mi = lax.broadcasted_iota(jnp.int32, (Mb, blocks[0][2]), 0)
    ci = lax.broadcasted_iota(jnp.int32, (Mb, blocks[0][2]), 1)
    if S == 1:
        dist = mi + H - ci
    else:
        (ms, mt), (cs, cr) = _divmod_const(mi, Ls), _divmod_const(ci, E)
        dist = jnp.where(ms == cs, mt + H - cr, -1)

    row = lax.broadcasted_iota(jnp.int32, (M, 1), 0)
    pos = pos0 + t * Ls + (_divmod_const(row, Ls)[1] if S > 1 else row)
    ys = []
    for gi, w in enumerate(POOL_WINDOWS[:G]):
        band = jnp.where((dist >= 0) & (dist < w), 1.0, 0.0).astype(BF16)
        sl = slice(gi * P, (gi + 1) * P)
        sums = [_bdot(band, hi[c0:c0 + cb, sl]) + _bdot(band, lo[c0:c0 + cb, sl]) for _, c0, cb in blocks]
        sg = sums[0] if len(sums) == 1 else jnp.concatenate(sums, axis=0)
        cnt = jnp.minimum(pos + 1, w).astype(F32)
        d = sg / cnt - xn[:, sl]
        ys.append(_bdot(d.astype(BF16), wp_ref[gi]))
    y = (jnp.concatenate(ys, axis=1) + bg_ref[...]) * sc_ref[...]
    x2 = x + y
    xo_ref[...] = x2.reshape(S, Ls, D)

    xn2 = _rmsnorm(x2, mg_ref[...])
    _put_token_tiles(xn_o_ref, (), xn2)
    NL = wr_ref.shape[1]
    lane = lax.broadcasted_iota(jnp.int32, (M, NL), 1)
    logits = jnp.where(lane < n_exp, _bdot(xn2.astype(BF16), wr_ref[...]) + br_ref[...], -jnp.inf)
    v0 = jnp.max(logits, axis=1, keepdims=True)
    i0 = jnp.min(jnp.where(logits == v0, lane, NL), axis=1, keepdims=True)
    rest = jnp.where(lane == i0, -jnp.inf, logits)
    v1 = jnp.max(rest, axis=1, keepdims=True)
    i1 = jnp.min(jnp.where(rest == v1, lane, NL), axis=1, keepdims=True)
    e1 = jnp.exp(v1 - v0)
    den = 1.0 + e1
    ridx_ref[...] = jnp.where(lane == 0, i0, jnp.where(lane == 1, i1, 0)).reshape(S, Ls, NL)
    rp_ref[...] = jnp.where(lane == 0, 1.0 / den, jnp.where(lane == 1, e1 / den, 0.0)).reshape(S, Ls, NL)


def _pool_block(x, pool_prev, p, *, S, Ls, pos0):
    B, L, D = x.shape
    assert B % S == 0 and L % Ls == 0 and Ls % V7X_SUBLANES == 0
    assert D == V7X_SUBLANES * V7X_LANES, "the expert dispatch stores one (8, 128) tile per token"
    HS = 2 * V7X_SUBLANES
    assert POOL_STATE < HS
    H = V7X_LANES if S == 1 else pl.next_power_of_2(Ls + HS) - Ls
    assert H >= HS and (S * (H + Ls)) % V7X_LANES == 0
    NL = p["wr"].shape[1]
    pprev = jnp.pad(pool_prev, ((0, 0), (HS - POOL_STATE, 0), (0, 0)))
    const = lambda shape: pl.BlockSpec(shape, lambda b, t: (0,) * len(shape))
    kern = functools.partial(_pool_kernel, S=S, Ls=Ls, H=H, n_t=L // Ls, pos0=pos0, n_exp=p["n_exp"])
    tok = lambda last: pl.BlockSpec((S, Ls, last), lambda b, t: (b, t, 0))
    x2, xn2, ridx, rp, pool_new = pl.pallas_call(
        kern,
        grid=(B // S, L // Ls),
        in_specs=[tok(D), pl.BlockSpec((S, HS, D), lambda b, t: (b, 0, 0)),
                  const((1, D)), const(p["wp"].shape), const((1, D)), const((1, D)), const((1, D)),
                  const(p["wr"].shape), const((1, NL))],
        out_specs=[tok(D),
                   pl.BlockSpec((S * Ls, V7X_SUBLANES, V7X_LANES), lambda b, t: (b * (L // Ls) + t, 0, 0)),
                   tok(NL), tok(NL), pl.BlockSpec((S, HS, D), lambda b, t: (b, 0, 0))],
        out_shape=[
            jax.ShapeDtypeStruct((B, L, D), F32),
            jax.ShapeDtypeStruct((B * L, V7X_SUBLANES, V7X_LANES), F32),
            jax.ShapeDtypeStruct((B, L, NL), jnp.int32),
            jax.ShapeDtypeStruct((B, L, NL), F32),
            jax.ShapeDtypeStruct((B, HS, D), F32),
        ],
        scratch_shapes=[pltpu.VMEM((S, H + Ls, D), F32)],
        compiler_params=pltpu.CompilerParams(
            dimension_semantics=("arbitrary", "arbitrary"), vmem_limit_bytes=V7X_VMEM_LIMIT_BYTES),
        name="pool_block",
    )(x, pprev, p["g"], p["wp"], p["bg"], p["sc"], p["mg"], p["wr"], p["br"])
    return x2, xn2, ridx, rp, pool_new[:, HS - POOL_STATE:, :]


def _route(ridx, n_exp, TM):
    N = ridx.shape[0]
    NA = TOP_K * N
    flat_e = ridx[:, :TOP_K].T.reshape(NA)
    order = jnp.argsort(flat_e, stable=True).astype(jnp.int32)
    counts = jnp.sum((flat_e[:, None] == jnp.arange(n_exp, dtype=jnp.int32)[None, :]).astype(jnp.int32), axis=0)
    tiles_per = (counts + TM - 1) // TM
    tile_end = jnp.cumsum(tiles_per)
    n_tiles = tile_end[-1]
    row_start = jnp.cumsum(counts) - counts
    max_tiles = pl.cdiv(NA, TM) + n_exp
    tid = jnp.arange(max_tiles, dtype=jnp.int32)
    te = jnp.minimum(jnp.searchsorted(tile_end, tid, side="right"), n_exp - 1).astype(jnp.int32)
    te = jnp.where(tid < n_tiles, te, te[jnp.maximum(n_tiles - 1, 0)])
    j = tid - (tile_end - tiles_per)[te]
    nv = jnp.where(tid < n_tiles, jnp.clip(counts[te] - j * TM, 0, TM), 0).astype(jnp.int32)
    lane = jnp.arange(TM, dtype=jnp.int32)[None, :]
    rows = (row_start[te] + j * TM)[:, None] + lane
    asg = jnp.where(lane < nv[:, None], order[jnp.clip(rows, 0, NA - 1)], 0)
    return te, nv, asg.reshape(max_tiles, 1, TM)


def _moe_sparse_kernel(te_ref, nv_ref, asg_ref, asg_next_ref, xn_hbm, wg_ref, wu_ref, wd_ref, out_hbm,
                       xbuf, ybuf, gsem, ssem, *, TM, N):
    i = pl.program_id(0)
    n = pl.num_programs(0)
    slot = lax.rem(i, 2)

    def gather_row(r, token, s):
        return pltpu.make_async_copy(xn_hbm.at[token], xbuf.at[s, r], gsem.at[s])

    def scatter_row(r, dest, s):
        return pltpu.make_async_copy(ybuf.at[s, r], out_hbm.at[dest], ssem.at[s])

    def start_gather(idx_ref, s):
        def body(r, c):
            a = idx_ref[0, 0, r]
            for _ in range(TOP_K - 1):
                a = jnp.where(a >= N, a - N, a)
            gather_row(r, a, s).start()
            return c
        lax.fori_loop(0, TM, body, 0, unroll=8)

    def wait_rows(make, count, s):
        def wait_full():
            for _ in range(TM):
                make(0, 0, s).wait()

        if isinstance(count, int):
            assert count == TM
            return wait_full()
        pl.when(count == TM)(wait_full)

        @pl.when(count < TM)
        def _():
            def body(r, c):
                make(0, 0, s).wait()
                return c
            lax.fori_loop(0, count, body, 0)

    @pl.when(i == 0)
    def _():
        start_gather(asg_ref, 0)

    wait_rows(gather_row, TM, slot)

    @pl.when(i + 1 < n)
    def _():
        start_gather(asg_next_ref, 1 - slot)

    @pl.when(i >= 2)
    def _():
        wait_rows(scatter_row, nv_ref[jnp.maximum(i - 2, 0)], slot)

    nv = nv_ref[i]

    @pl.when(nv > 0)
    def _():
        xb = _get_token_tiles(xbuf, (slot,)).astype(BF16)
        y = _swiglu(xb, lambda a, b: wg_ref[0, :, a:b], lambda a, b: wu_ref[0, :, a:b],
                    lambda a, b: wd_ref[0, a:b, :], wg_ref.shape[2])
        _put_token_tiles(ybuf, (slot,), y)

        def body(r, c):
            scatter_row(r, asg_ref[0, 0, r], slot).start()
            return c
        lax.fori_loop(0, nv, body, 0)

    @pl.when(i == n - 1)
    def _():
        @pl.when(i >= 1)
        def _():
            wait_rows(scatter_row, nv_ref[jnp.maximum(i - 1, 0)], 1 - slot)
        wait_rows(scatter_row, nv, slot)


def _moe_sparse(xn, ridx, p, *, TM):
    N = xn.shape[0]
    tile = xn.shape[1:]
    n_exp, D, Fd = p["wg"].shape
    assert tile == (V7X_SUBLANES, V7X_LANES) and D == V7X_SUBLANES * V7X_LANES
    te, nv, asg = _route(ridx, n_exp, TM)
    max_tiles = te.shape[0]
    wspec = lambda shape: pl.BlockSpec((1,) + shape, lambda i, te, nv: (te[i], 0, 0))
    smem_tile = lambda fn: pl.BlockSpec((1, 1, TM), fn, memory_space=pltpu.SMEM)
    kern = functools.partial(_moe_sparse_kernel, TM=TM, N=N)
    return pl.pallas_call(
        kern,
        grid_spec=pltpu.PrefetchScalarGridSpec(
            num_scalar_prefetch=2,
            grid=(max_tiles,),
            in_specs=[smem_tile(lambda i, te, nv: (i, 0, 0)),
                      smem_tile(lambda i, te, nv: (jnp.minimum(i + 1, max_tiles - 1), 0, 0)),
                      pl.BlockSpec(memory_space=pl.ANY),
                      wspec((D, Fd)), wspec((D, Fd)), wspec((Fd, D))],
            out_specs=pl.BlockSpec(memory_space=pl.ANY),
            scratch_shapes=[pltpu.VMEM((2, TM) + tile, F32), pltpu.VMEM((2, TM) + tile, F32),
                            pltpu.SemaphoreType.DMA((2,)), pltpu.SemaphoreType.DMA((2,))],
        ),
        out_shape=jax.ShapeDtypeStruct((TOP_K * N,) + tile, F32),
        compiler_params=pltpu.CompilerParams(
            dimension_semantics=("arbitrary",), vmem_limit_bytes=V7X_VMEM_LIMIT_BYTES),
        name="moe_sparse",
    )(te, nv, asg, asg, xn, p["wg"], p["wu"], p["wd"])


def _final_kernel(x_ref, rp_ref, *refs):
    y_refs, g_ref, o_ref = refs[:TOP_K], refs[TOP_K], refs[TOP_K + 1]
    rp = rp_ref[...]
    moe = rp[:, 0:1] * _get_token_tiles(y_refs[0], ())
    for k in range(1, TOP_K):
        moe = moe + rp[:, k:k + 1] * _get_token_tiles(y_refs[k], ())
    o_ref[...] = _rmsnorm(x_ref[...] + moe, g_ref[...])


def _final_block(x, rp, y, g, *, T, off, n_all):
    N, D = x.shape
    NL = rp.shape[1]
    assert N % T == 0 and off % T == 0 and n_all % T == 0
    tok = lambda last: pl.BlockSpec((T, last), lambda i: (i, 0))
    yk = lambda k: pl.BlockSpec((T,) + y.shape[1:], lambda i: ((k * n_all + off) // T + i, 0, 0))
    return pl.pallas_call(
        _final_kernel,
        grid=(N // T,),
        in_specs=[tok(D), tok(NL)] + [yk(k) for k in range(TOP_K)] + [pl.BlockSpec((1, D), lambda i: (0, 0))],
        out_specs=tok(D),
        out_shape=jax.ShapeDtypeStruct((N, D), F32),
        compiler_params=pltpu.CompilerParams(
            dimension_semantics=("arbitrary",), vmem_limit_bytes=V7X_VMEM_LIMIT_BYTES),
        name="final_norm",
    )(x, rp, *([y] * TOP_K), g)


def _block_diag_groups(w, group):
    n, k, _ = w.shape
    eye = jnp.eye(group, dtype=w.dtype)
    wg = w.reshape(n // group, group, k, k)
    return jnp.einsum("qaij,ab->qaibj", wg, eye).reshape(n // group, group * k, group * k)


def _row(v):
    return v.reshape(1, -1).astype(F32)


def _prep_lru(j, norm_g, w_x, w_y, conv_w, conv_b, w_a, b_a, w_i, b_i, lam, w_out):
    nblk, blk = w_a.shape[1], w_a.shape[2]
    group = max(1, min(nblk, V7X_MXU_DIM // blk))
    assert nblk % group == 0
    wai = jnp.concatenate([_block_diag_groups(w_a[j], group), _block_diag_groups(w_i[j], group)], axis=2)
    return dict(g=_row(norm_g[j]), wxy=jnp.concatenate([w_x[j], w_y[j]], axis=1).astype(BF16),
                cw=conv_w[j].astype(F32), cb=_row(conv_b[j]), wai=wai.astype(BF16), ba=_row(b_a[j]),
                bi=_row(b_i[j]), lam=_row(lam[j]), wout=w_out[j].astype(BF16))


def _prep_pool(j, norm_g, w, b, scale, moe_g, w_router, b_router):
    D, n_exp = w_router.shape[1], w_router.shape[2]
    NL = V7X_LANES
    assert n_exp <= NL
    wr = jnp.zeros((D, NL), BF16).at[:, :n_exp].set(w_router[j].astype(BF16))
    br = jnp.zeros((1, NL), F32).at[:, :n_exp].set(b_router[j].astype(F32))
    return dict(g=_row(norm_g[j]), wp=w[j].astype(BF16), bg=_row(b[j]), sc=_row(scale[j]), mg=_row(moe_g[j]),
                wr=wr, br=br, n_exp=n_exp)


def _largest_tile(n, cap):
    t = cap
    while n % t:
        t //= 2
    return t


def _token_tile(n):
    return _largest_tile(n, 512)


def _trunk_to_router(x, pos0, conv_in, rnn_in, pool_in, lru, ffn, pool):
    B, L, D = x.shape
    S, Ls = (B, L) if B * L <= 512 else (1, _largest_tile(L, 256))
    x1, conv_new, h_new = _lru_block(x, conv_in, rnn_in, lru, S=S, Ls=Ls, pos0=pos0)
    N = B * L
    x1 = _ffn_block(x1.reshape(N, D), ffn, T=_token_tile(N)).reshape(B, L, D)
    x2, xn2, ridx, rp, pool_new = _pool_block(x1, pool_in, pool, S=S, Ls=Ls, pos0=pos0)
    NL = ridx.shape[-1]
    flat = dict(x2=x2.reshape(N, D), xn2=xn2, ridx=ridx.reshape(N, NL), rp=rp.reshape(N, NL))
    return flat, (conv_new, h_new.reshape(B, -1), pool_new)


def kernel(x_prompt, x_sample, cache_conv, state_rnn, cache_pool, lru_norm_g, lru_w_x, lru_w_y, lru_conv_w, lru_conv_b, lru_w_a, lru_b_a, lru_w_i, lru_b_i, lru_lambda, lru_w_out, ffn_norm_g, ffn_w_gate, ffn_w_up, ffn_w_down, pool_norm_g, pool_w, pool_b, pool_scale, moe_norm_g, moe_w_router, moe_b_router, moe_w_gate, moe_w_up, moe_w_down, final_norm_g):
    assert lru_norm_g.shape[0] == 1 and pool_norm_g.shape[0] == 1, "one layer of each mixer type"
    bp, lp, D = x_prompt.shape
    lru = _prep_lru(0, lru_norm_g, lru_w_x, lru_w_y, lru_conv_w, lru_conv_b, lru_w_a, lru_b_a, lru_w_i, lru_b_i,
                    lru_lambda, lru_w_out)
    ffn = dict(g=_row(ffn_norm_g[0]), wg=ffn_w_gate[0].astype(BF16), wu=ffn_w_up[0].astype(BF16),
               wd=ffn_w_down[0].astype(BF16))
    pool = _prep_pool(0, pool_norm_g, pool_w, pool_b, pool_scale, moe_norm_g, moe_w_router, moe_b_router)
    moe = dict(wg=moe_w_gate[0].astype(BF16), wu=moe_w_up[0].astype(BF16), wd=moe_w_down[0].astype(BF16))
    fg = _row(final_norm_g)

    R = lru_w_out.shape[1]
    zc = jnp.zeros((bp, CONV_WIDTH - 1, R), F32)
    zh = jnp.zeros((bp, R), F32)
    zp = jnp.zeros((bp, POOL_STATE, D), F32)
    fp, (cp, hp, pp) = _trunk_to_router(x_prompt, 0, zc, zh, zp, lru, ffn, pool)
    fs, (cs, hs, ps) = _trunk_to_router(x_sample, PAST_LEN, cache_conv[0], state_rnn[0], cache_pool[0],
                                        lru, ffn, pool)

    n_p, n_s = fp["x2"].shape[0], fs["x2"].shape[0]
    n_all = n_p + n_s
    y = _moe_sparse(jnp.concatenate([fp["xn2"], fs["xn2"]], axis=0),
                    jnp.concatenate([fp["ridx"], fs["ridx"]], axis=0), moe, TM=512)
    yp = _final_block(fp["x2"], fp["rp"], y, fg, T=_token_tile(n_p), off=0, n_all=n_all)
    ys = _final_block(fs["x2"], fs["rp"], y, fg, T=_token_tile(n_s), off=n_p, n_all=n_all)
    return (yp.reshape(x_prompt.shape), ys.reshape(x_sample.shape),
            cp[None], cs[None], hp[None], hs[None], pp[None], ps[None])
```
